```python
import jax, jax.numpy as jnp
from jax import lax
import numpy as np

D_MODEL = 4096
BATCH = 4
SEQ = 4096
DEPTH = 4

HEAD_DIM = 128
ATTN_WIDTH = D_MODEL // 2
N_ATTN_HEADS = ATTN_WIDTH // HEAD_DIM
LRU_WIDTH = D_MODEL - ATTN_WIDTH
LRU_BLOCKS = 16
LRU_BLOCK = LRU_WIDTH // LRU_BLOCKS
CONV_WIDTH = 4
LRU_C = 8.0
FFN_DIM = 5120
DILATED_CONFIGS = ((128, 1), (512, 4), (2048, 16))
Q_BLOCK = 128
ROPE_THETA = 10000.0
NORM_EPS = 1e-6
N_MOD = 9
IN_WIDTH = 3 * ATTN_WIDTH + 2 * LRU_WIDTH

kernel_name = "hymba_rglru_dilated_macaron_adaln"


def _rms(x):
    x32 = x.astype(jnp.float32)
    y = x32 * lax.rsqrt(jnp.mean(x32 * x32, axis=-1, keepdims=True) + NORM_EPS)
    return y.astype(x.dtype)


def _modulate(h, shift, scale):
    return h * (1.0 + scale[:, None, :]) + shift[:, None, :]


def _swiglu(h, w_gate, w_up, w_down):
    return (jax.nn.silu(h @ w_gate) * (h @ w_up)) @ w_down


def _rope(x, pos):
    half = HEAD_DIM // 2
    inv = ROPE_THETA ** (-jnp.arange(half, dtype=jnp.float32) / half)
    ang = pos.astype(jnp.float32)[:, None] * inv[None, :]
    cos = jnp.cos(ang)[None, :, None, :]
    sin = jnp.sin(ang)[None, :, None, :]
    x1, x2 = x[..., :half], x[..., half:]
    return jnp.concatenate([x1 * cos - x2 * sin, x2 * cos + x1 * sin], axis=-1)


def _to_strided(x, d):
    B, S = x.shape[:2]
    rest = x.shape[2:]
    return x.reshape((B, S // d, d) + rest).swapaxes(1, 2).reshape((B * d, S // d) + rest)


def _from_strided(x, d, B):
    N, L = x.shape[:2]
    rest = x.shape[2:]
    return x.reshape((B, d, L) + rest).swapaxes(1, 2).reshape((B, L * d) + rest)


def _window_attn(q, k, v, window):
    N, L, H, Dh = q.shape
    nb = -(-L // Q_BLOCK)
    Lp = nb * Q_BLOCK
    pad = Lp - L
    qp = jnp.pad(q, ((0, 0), (0, pad), (0, 0), (0, 0)))
    kp = jnp.pad(k, ((0, 0), (Q_BLOCK, pad), (0, 0), (0, 0)))
    vp = jnp.pad(v, ((0, 0), (Q_BLOCK, pad), (0, 0), (0, 0)))
    qb = qp.reshape(N, nb, Q_BLOCK, H, Dh)
    def band(t):
        prev = t[:, :Lp].reshape(N, nb, Q_BLOCK, H, Dh)
        cur = t[:, Q_BLOCK:].reshape(N, nb, Q_BLOCK, H, Dh)
        return jnp.concatenate([prev, cur], axis=2)
    kb, vb = band(kp), band(vp)
    s = jnp.einsum('nbqhd,nbkhd->nbhqk', qb, kb) * (HEAD_DIM ** -0.5)
    qi = jnp.arange(Q_BLOCK)[:, None]
    kj = jnp.arange(2 * Q_BLOCK)[None, :]
    dist = qi + Q_BLOCK - kj
    local = (dist >= 0) & (dist <= window)
    key_abs = jnp.arange(nb)[:, None, None] * Q_BLOCK - Q_BLOCK + kj[None]
    mask = local[None] & (key_abs >= 0)
    s = jnp.where(mask[None, :, None], s, -jnp.inf)
    lse = jax.nn.logsumexp(s, axis=-1)
    p = jnp.exp(s - lse[..., None])
    o = jnp.einsum('nbhqk,nbkhd->nbqhd', p, vb).reshape(N, Lp, H, Dh)[:, :L]
    lse = lse.transpose(0, 1, 3, 2).reshape(N, Lp, H)[:, :L]
    return o, lse


def _dilated_attention(q, k, v):
    B = q.shape[0]
    outs, lses = [], []
    for window, d in DILATED_CONFIGS:
        o, lse = _window_attn(_to_strided(q, d), _to_strided(k, d), _to_strided(v, d), window // d)
        outs.append(_from_strided(o, d, B))
        lses.append(_from_strided(lse, d, B))
    alpha = jax.nn.softmax(jnp.stack(lses, axis=0), axis=0)
    return jnp.einsum('cbsh,cbshd->bshd', alpha, jnp.stack(outs, axis=0))


def _lin_combine(e1, e2):
    a1, b1 = e1
    a2, b2 = e2
    return a1 * a2, a2 * b1 + b2


def _rglru_branch(xb, gb, conv_w, conv_b, w_rg_a, b_rg_a, w_rg_x, b_rg_x, lru_lambda):
    B, S, W = xb.shape
    x32 = xb.astype(jnp.float32)
    xp = jnp.pad(x32, ((0, 0), (CONV_WIDTH - 1, 0), (0, 0)))
    xc = conv_b.astype(jnp.float32)
    for tap in range(CONV_WIDTH):
        xc = xc + xp[:, tap:tap + S] * conv_w[tap].astype(jnp.float32)
    xh = xc.reshape(B, S, LRU_BLOCKS, LRU_BLOCK)
    r = jax.nn.sigmoid(jnp.einsum('bsgi,gij->bsgj', xh, w_rg_a.astype(jnp.float32)).reshape(B, S, W)
                       + b_rg_a.astype(jnp.float32))
    i = jax.nn.sigmoid(jnp.einsum('bsgi,gij->bsgj', xh, w_rg_x.astype(jnp.float32)).reshape(B, S, W)
                       + b_rg_x.astype(jnp.float32))
    log_a = -LRU_C * r * jax.nn.softplus(-lru_lambda.astype(jnp.float32))
    a = jnp.exp(log_a)
    b = jnp.sqrt(-jnp.expm1(2.0 * log_a)) * (i * xc)
    _, h = lax.associative_scan(_lin_combine, (a, b), axis=1)
    return h * jax.nn.gelu(gb.astype(jnp.float32))


def _mixer(h, w_in, conv_w, conv_b, w_rg_a, b_rg_a, w_rg_x, b_rg_x, lru_lambda,
           g_attn_out, g_lru_out, w_out):
    B, S, _ = h.shape
    z = h @ w_in
    q, k, v, xb, gb = jnp.split(
        z, [ATTN_WIDTH, 2 * ATTN_WIDTH, 3 * ATTN_WIDTH, 3 * ATTN_WIDTH + LRU_WIDTH], axis=-1)
    heads = lambda t: t.reshape(B, S, N_ATTN_HEADS, HEAD_DIM).astype(jnp.float32)
    pos = jnp.arange(S)
    q = _rope(heads(q), pos)
    k = _rope(heads(k), pos)
    attn = _dilated_attention(q, k, heads(v)).reshape(B, S, ATTN_WIDTH)
    lru = _rglru_branch(xb, gb, conv_w, conv_b, w_rg_a, b_rg_a, w_rg_x, b_rg_x, lru_lambda)
    y = jnp.concatenate([_rms(attn) * g_attn_out, _rms(lru) * g_lru_out], axis=-1)
    return y.astype(h.dtype) @ w_out


def setup_inputs(seed: int = 0) -> dict:
    key = jax.random.key(seed)
    ks = jax.random.split(key, 24)
    f32 = jnp.float32
    nrm = lambda k, shape, s: jax.random.normal(k, shape, f32) * s
    D, L, F = D_MODEL, DEPTH, FFN_DIM
    u = jax.random.uniform(ks[13], (L, LRU_WIDTH), f32, 0.9, 0.999)
    p = u ** (1.0 / LRU_C)
    lru_lambda = jnp.log(p) - jnp.log1p(-p)
    return {
        "x": nrm(ks[0], (BATCH, SEQ, D), 1.0),
        "c": nrm(ks[1], (BATCH, D), 1.0),
        "w_ada": nrm(ks[2], (D, N_MOD * D), D ** -0.5),
        "b_ada": nrm(ks[3], (N_MOD * D,), 0.01),
        "ada_emb": nrm(ks[4], (L, N_MOD, D), 0.1),
        "w_ffn1_gate": nrm(ks[5], (L, D, F), D ** -0.5),
        "w_ffn1_up": nrm(ks[6], (L, D, F), D ** -0.5),
        "w_ffn1_down": nrm(ks[7], (L, F, D), F ** -0.5),
        "w_in": nrm(ks[8], (L, D, IN_WIDTH), D ** -0.5),
        "conv_w": nrm(ks[9], (L, CONV_WIDTH, LRU_WIDTH), CONV_WIDTH ** -0.5),
        "conv_b": nrm(ks[10], (L, LRU_WIDTH), 0.01),
        "w_rg_a": nrm(ks[11], (L, LRU_BLOCKS, LRU_BLOCK, LRU_BLOCK), LRU_BLOCK ** -0.5),
        "b_rg_a": nrm(ks[12], (L, LRU_WIDTH), 0.01),
        "w_rg_x": nrm(ks[14], (L, LRU_BLOCKS, LRU_BLOCK, LRU_BLOCK), LRU_BLOCK ** -0.5),
        "b_rg_x": nrm(ks[15], (L, LRU_WIDTH), 0.01),
        "lru_lambda": lru_lambda,
        "g_attn_out": 1.0 + nrm(ks[16], (L, ATTN_WIDTH), 0.1),
        "g_lru_out": 1.0 + nrm(ks[17], (L, LRU_WIDTH), 0.1),
        "w_out": nrm(ks[18], (L, D, D), D ** -0.5),
        "w_ffn2_gate": nrm(ks[19], (L, D, F), D ** -0.5),
        "w_ffn2_up": nrm(ks[20], (L, D, F), D ** -0.5),
        "w_ffn2_down": nrm(ks[21], (L, F, D), F ** -0.5),
        "g_final": 1.0 + nrm(ks[22], (D,), 0.1),
    }


def reference(x, c, w_ada, b_ada, ada_emb, w_ffn1_gate, w_ffn1_up, w_ffn1_down,
              w_in, conv_w, conv_b, w_rg_a, b_rg_a, w_rg_x, b_rg_x, lru_lambda,
              g_attn_out, g_lru_out, w_out, w_ffn2_gate, w_ffn2_up, w_ffn2_down, g_final):
    B = x.shape[0]
    mod = (jax.nn.silu(c) @ w_ada + b_ada).reshape(B, N_MOD, D_MODEL)
    for l in range(DEPTH):
        m = mod + ada_emb[l][None]
        sh1, sc1, gt1, sh2, sc2, gt2, sh3, sc3, gt3 = [m[:, j] for j in range(N_MOD)]
        h = _modulate(_rms(x), sh1, sc1)
        x = x + 0.5 * gt1[:, None, :] * _swiglu(h, w_ffn1_gate[l], w_ffn1_up[l], w_ffn1_down[l])
        h = _modulate(_rms(x), sh2, sc2)
        y = _mixer(h, w_in[l], conv_w[l], conv_b[l], w_rg_a[l], b_rg_a[l], w_rg_x[l], b_rg_x[l],
                   lru_lambda[l], g_attn_out[l], g_lru_out[l], w_out[l])
        x = x + gt2[:, None, :] * y
        h = _modulate(_rms(x), sh3, sc3)
        x = x + 0.5 * gt3[:, None, :] * _swiglu(h, w_ffn2_gate[l], w_ffn2_up[l], w_ffn2_down[l])
    return _rms(x) * g_final
```

```python
import functools
import math

import jax
import jax.numpy as jnp
from jax import lax
from jax.experimental import pallas as pl
from jax.experimental.pallas import tpu as pltpu

F32 = jnp.float32
BF16 = jnp.bfloat16

HEAD_DIM = 128
LRU_BLOCK = 128
CONV_WIDTH = 4
LRU_C = 8.0
DILATIONS = (1, 4, 16)
Q_BLOCK = 128
ROPE_THETA = 10000.0
NORM_EPS = 1e-6
N_MOD = 9

LANE = 128
VMEM_LIMIT_BYTES = 56 * 1024 * 1024
NEG_BIG = -1e30


def _cparams(*semantics):
    return pltpu.CompilerParams(dimension_semantics=semantics,
                                vmem_limit_bytes=VMEM_LIMIT_BYTES)


def _tile(n, pref):
    if n <= pref:
        return n
    t = (pref // LANE) * LANE
    while n % t:
        t -= LANE
    return t


def _ada_kernel(c_ref, w_ref, b_ref, e_ref, o_ref, *, depth):
    c = c_ref[...]
    sc = (c * jax.nn.sigmoid(c)).astype(BF16)
    acc = jnp.dot(sc, w_ref[...].astype(BF16), preferred_element_type=F32)
    base = acc + b_ref[...]
    for l in range(depth):
        o_ref[l] = base + e_ref[l:l + 1, :]


def _ada_mods(c, w_ada, b_ada, ada_emb):
    B, D = c.shape
    depth = ada_emb.shape[0]
    N = w_ada.shape[1]
    rows = 16
    c_pad = jnp.pad(c, ((0, rows - B), (0, 0)))
    tn = _tile(N, 512)
    out = pl.pallas_call(
        functools.partial(_ada_kernel, depth=depth),
        grid=(N // tn,),
        in_specs=[
            pl.BlockSpec((rows, D), lambda j: (0, 0)),
            pl.BlockSpec((D, tn), lambda j: (0, j)),
            pl.BlockSpec((1, tn), lambda j: (0, j)),
            pl.BlockSpec((depth, tn), lambda j: (0, j)),
        ],
        out_specs=pl.BlockSpec((depth, rows, tn), lambda j: (0, 0, j)),
        out_shape=jax.ShapeDtypeStruct((depth, rows, N), F32),
        compiler_params=_cparams("arbitrary"),
        name="ada_mods",
    )(c_pad, w_ada, b_ada.reshape(1, N), ada_emb.reshape(depth, N))
    return out[:, :B].reshape(depth, B, N_MOD, 1, D)


def _norm_mod_kernel(x_ref, sh_ref, sc_ref, o_ref):
    x = x_ref[...]
    ms = jnp.mean(x * x, axis=-1, keepdims=True)
    y = x * lax.rsqrt(ms + NORM_EPS)
    o_ref[...] = (y * (1.0 + sc_ref[...]) + sh_ref[...]).astype(BF16)


def _norm_mod(x, shift, scale, S):
    M, D = x.shape
    tr = _tile(S, 256)
    return pl.pallas_call(
        _norm_mod_kernel,
        grid=(M // tr,),
        in_specs=[
            pl.BlockSpec((tr, D), lambda i: (i, 0)),
            pl.BlockSpec((None, 1, D), lambda i: ((i * tr) // S, 0, 0)),
            pl.BlockSpec((None, 1, D), lambda i: ((i * tr) // S, 0, 0)),
        ],
        out_specs=pl.BlockSpec((tr, D), lambda i: (i, 0)),
        out_shape=jax.ShapeDtypeStruct((M, D), BF16),
        compiler_params=_cparams("arbitrary"),
        name="norm_mod",
    )(x, shift, scale)


def _swiglu_up_kernel(h_ref, wg_ref, wu_ref, o_ref):
    h = h_ref[...]
    g = jnp.dot(h, wg_ref[...], preferred_element_type=F32)
    u = jnp.dot(h, wu_ref[...], preferred_element_type=F32)
    o_ref[...] = (g * jax.nn.sigmoid(g) * u).astype(BF16)


def _swiglu_up(h, wg, wu):
    M, D = h.shape
    F = wg.shape[1]
    tm, tn = _tile(M, 1024), _tile(F, 512)
    return pl.pallas_call(
        _swiglu_up_kernel,
        grid=(M // tm, F // tn),
        in_specs=[
            pl.BlockSpec((tm, D), lambda i, j: (i, 0)),
            pl.BlockSpec((D, tn), lambda i, j: (0, j)),
            pl.BlockSpec((D, tn), lambda i, j: (0, j)),
        ],
        out_specs=pl.BlockSpec((tm, tn), lambda i, j: (i, j)),
        out_shape=jax.ShapeDtypeStruct((M, F), BF16),
        compiler_params=_cparams("arbitrary", "arbitrary"),
        name="swiglu_up",
    )(h, wg, wu)


def _proj_residual_kernel(a_ref, w_ref, x_ref, g_ref, o_ref, *, coef):
    acc = jnp.dot(a_ref[...], w_ref[...], preferred_element_type=F32)
    o_ref[...] = x_ref[...] + (coef * g_ref[...]) * acc


def _proj_residual(a, w, x, gate, coef, S):
    M, K = a.shape
    N = w.shape[1]
    tm, tn = _tile(M, 1024), _tile(N, 512)
    tm = math.gcd(tm, S)
    return pl.pallas_call(
        functools.partial(_proj_residual_kernel, coef=coef),
        grid=(M // tm, N // tn),
        in_specs=[
            pl.BlockSpec((tm, K), lambda i, j: (i, 0)),
            pl.BlockSpec((K, tn), lambda i, j: (0, j)),
            pl.BlockSpec((tm, tn), lambda i, j: (i, j)),
            pl.BlockSpec((None, 1, tn), lambda i, j: ((i * tm) // S, 0, j)),
        ],
        out_specs=pl.BlockSpec((tm, tn), lambda i, j: (i, j)),
        out_shape=jax.ShapeDtypeStruct((M, N), F32),
        input_output_aliases={2: 0},
        compiler_params=_cparams("arbitrary", "arbitrary"),
        name="proj_residual",
    )(a, w, x, gate)


def _in_proj_kernel(h_ref, w_ref, o_ref, *, nsub):
    acc = jnp.dot(h_ref[...], w_ref[...], preferred_element_type=F32)
    for k in range(nsub):
        o_ref[k] = acc[:, k * LANE:(k + 1) * LANE]


def _in_proj(h, w):
    M, D = h.shape
    N = w.shape[1]
    tm, tn = _tile(M, 1024), _tile(N, 512)
    nsub = tn // LANE
    return pl.pallas_call(
        functools.partial(_in_proj_kernel, nsub=nsub),
        grid=(M // tm, N // tn),
        in_specs=[
            pl.BlockSpec((tm, D), lambda i, j: (i, 0)),
            pl.BlockSpec((D, tn), lambda i, j: (0, j)),
        ],
        out_specs=pl.BlockSpec((nsub, tm, LANE), lambda i, j: (j, i, 0)),
        out_shape=jax.ShapeDtypeStruct((N // LANE, M, LANE), F32),
        compiler_params=_cparams("arbitrary", "arbitrary"),
        name="in_proj",
    )(h, w)


def _band_attention(q_src, k_src, v_src, o_dst, l_dst, *, period, S, group):
    QB = Q_BLOCK
    qi = lax.broadcasted_iota(jnp.int32, (QB, QB), 0)
    kj = lax.broadcasted_iota(jnp.int32, (QB, QB), 1)
    cur_mask = kj <= qi
    diff = kj - qi
    nt_dims = (((1,), (1,)), ((), ()))

    def body(c, carry):
        base = pl.multiple_of(c * (group * QB), group * QB)
        for i in range(group):
            r0 = base + i * QB
            q = q_src[pl.ds(r0, QB), :].astype(BF16)
            kc = k_src[pl.ds(r0 + QB, QB), :].astype(BF16)
            kp = k_src[pl.ds(r0, QB), :].astype(BF16)
            vc = v_src[pl.ds(r0 + QB, QB), :].astype(BF16)
            vp = v_src[pl.ds(r0, QB), :].astype(BF16)
            s_c = lax.dot_general(q, kc, nt_dims, preferred_element_type=F32)
            s_p = lax.dot_general(q, kp, nt_dims, preferred_element_type=F32)
            blk = c * group + i
            thr = jnp.where((blk & (period - 1)) == 0, 2 * QB, 0)
            s_c = jnp.where(cur_mask, s_c, NEG_BIG)
            s_p = jnp.where(diff >= thr, s_p, NEG_BIG)
            m = jnp.maximum(jnp.max(s_c, axis=-1, keepdims=True),
                            jnp.max(s_p, axis=-1, keepdims=True))
            p_c = jnp.exp(s_c - m)
            p_p = jnp.exp(s_p - m)
            l = jnp.sum(p_c, axis=-1, keepdims=True) + jnp.sum(p_p, axis=-1, keepdims=True)
            o = (jnp.dot(p_c.astype(BF16), vc, preferred_element_type=F32)
                 + jnp.dot(p_p.astype(BF16), vp, preferred_element_type=F32))
            o_dst[pl.ds(r0, QB), :] = o / l
            l_dst[pl.ds(r0, QB), :] = jnp.broadcast_to(m + jnp.log(l), (QB, LANE))
        return carry

    lax.fori_loop(0, S // (group * QB), body, 0)


def _attn_kernel(q_ref, k_ref, v_ref, cos_ref, sin_ref, o_ref,
                 qn, kn, vn, qd, kd, vd, ld, ot, lt, oacc, lacc, *, S, group, chunk):
    QB = Q_BLOCK
    nblk = S // QB
    scale = HEAD_DIM ** -0.5
    zeros = jnp.zeros((QB, LANE), F32)
    kn[0:QB, :] = zeros
    vn[0:QB, :] = zeros
    kd[0:QB, :] = zeros
    vd[0:QB, :] = zeros

    def rope_body(c, carry):
        r0 = pl.multiple_of(c * chunk, chunk)
        cs = cos_ref[pl.ds(r0, chunk), :]
        sn = sin_ref[pl.ds(r0, chunk), :]
        q = q_ref[pl.ds(r0, chunk), :]
        k = k_ref[pl.ds(r0, chunk), :]
        qn[pl.ds(r0, chunk), :] = (q * cs + pltpu.roll(q, HEAD_DIM // 2, 1) * sn) * scale
        kn[pl.ds(r0 + QB, chunk), :] = k * cs + pltpu.roll(k, HEAD_DIM // 2, 1) * sn
        vn[pl.ds(r0 + QB, chunk), :] = v_ref[pl.ds(r0, chunk), :]
        return carry

    lax.fori_loop(0, S // chunk, rope_body, 0)

    _band_attention(qn, kn, vn, oacc, lacc, period=nblk, S=S, group=group)

    for d in DILATIONS[1:]:
        L = S // d
        for r in range(d):
            qd[r * L:(r + 1) * L, :] = qn[pl.ds(r, L, stride=d), :]
            kd[QB + r * L:QB + (r + 1) * L, :] = kn[pl.ds(QB + r, L, stride=d), :]
            vd[QB + r * L:QB + (r + 1) * L, :] = vn[pl.ds(QB + r, L, stride=d), :]
        _band_attention(qd, kd, vd, qd, ld, period=max(L // QB, 1), S=S, group=group)
        for r in range(d):
            ot[pl.ds(r, L, stride=d), :] = qd[r * L:(r + 1) * L, :]
            lt[pl.ds(r, L, stride=d), :] = ld[r * L:(r + 1) * L, :]

        def merge_body(c, carry):
            r0 = pl.multiple_of(c * chunk, chunk)
            la = lacc[pl.ds(r0, chunk), :]
            lb = lt[pl.ds(r0, chunk), :]
            m = jnp.maximum(la, lb)
            wa = jnp.exp(la - m)
            wb = jnp.exp(lb - m)
            den = wa + wb
            oacc[pl.ds(r0, chunk), :] = (wa * oacc[pl.ds(r0, chunk), :]
                                         + wb * ot[pl.ds(r0, chunk), :]) / den
            lacc[pl.ds(r0, chunk), :] = m + jnp.log(den)
            return carry

        lax.fori_loop(0, S // chunk, merge_body, 0)

    o_ref[...] = oacc[...]


def _rope_tables(S):
    half = HEAD_DIM // 2
    inv = ROPE_THETA ** (-jnp.arange(half, dtype=F32) / half)
    ang = jnp.arange(S, dtype=F32)[:, None] * inv[None, :]
    cos, sin = jnp.cos(ang), jnp.sin(ang)
    return jnp.concatenate([cos, cos], axis=-1), jnp.concatenate([-sin, sin], axis=-1)


def _attention(z, cos, sin, B, S, H):
    M = B * S
    seq_spec = lambda off: pl.BlockSpec((None, S, LANE), lambda b, h: (off + h, b, 0))
    tab_spec = pl.BlockSpec((S, LANE), lambda b, h: (0, 0))
    pad_rows = S + Q_BLOCK
    scratch = [
        pltpu.VMEM((S, LANE), F32), pltpu.VMEM((pad_rows, LANE), F32), pltpu.VMEM((pad_rows, LANE), F32),
        pltpu.VMEM((S, LANE), F32), pltpu.VMEM((pad_rows, LANE), F32), pltpu.VMEM((pad_rows, LANE), F32),
        pltpu.VMEM((S, LANE), F32),
        pltpu.VMEM((S, LANE), F32), pltpu.VMEM((S, LANE), F32),
        pltpu.VMEM((S, LANE), F32), pltpu.VMEM((S, LANE), F32),
    ]
    return pl.pallas_call(
        functools.partial(_attn_kernel, S=S, group=math.gcd(8, S // Q_BLOCK), chunk=_tile(S, 512)),
        grid=(B, H),
        in_specs=[seq_spec(0), seq_spec(H), seq_spec(2 * H), tab_spec, tab_spec],
        out_specs=pl.BlockSpec((None, S, LANE), lambda b, h: (h, b, 0)),
        out_shape=jax.ShapeDtypeStruct((H, M, LANE), F32),
        scratch_shapes=scratch,
        compiler_params=_cparams("arbitrary", "arbitrary"),
        name="dilated_attention",
    )(z, z, z, cos, sin)


def _gelu_tanh(x):
    return 0.5 * x * (1.0 + jnp.tanh(math.sqrt(2.0 / math.pi) * (x + 0.044715 * (x * x * x))))


def _lru_kernel(x_ref, g_ref, cw_ref, cb_ref, wa_ref, ba_ref, wx_ref, bx_ref, lam_ref, o_ref,
                xpad, a_s, b_s, *, S, chunk):
    SUB = 8
    seg = S // SUB
    xpad[0:SUB, :] = jnp.zeros((SUB, LANE), F32)
    xpad[SUB:SUB + S, :] = x_ref[...]

    lam = lam_ref[...]
    sp = jnp.maximum(-lam, 0.0) + jnp.log1p(jnp.exp(-jnp.abs(lam)))
    neg_c_sp = -LRU_C * sp
    wa = wa_ref[...].astype(BF16)
    wx = wx_ref[...].astype(BF16)
    cw = cw_ref[...]

    for c in range(S // chunk):
        r0 = c * chunk
        xc = cb_ref[...] + jnp.zeros((chunk, LANE), F32)
        for tap in range(CONV_WIDTH):
            off = SUB + r0 - (CONV_WIDTH - 1) + tap
            xc = xc + xpad[off:off + chunk, :] * cw[tap:tap + 1, :]
        xcb = xc.astype(BF16)
        r = jax.nn.sigmoid(jnp.dot(xcb, wa, preferred_element_type=F32) + ba_ref[...])
        i = jax.nn.sigmoid(jnp.dot(xcb, wx, preferred_element_type=F32) + bx_ref[...])
        log_a = neg_c_sp * r
        a = jnp.exp(log_a)
        t = 2.0 * log_a
        series = -t * (1.0 + t * (0.5 + t * (1.0 / 6.0 + t * (1.0 / 24.0))))
        one_minus = jnp.where(t > -0.02, series, 1.0 - a * a)
        a_s[r0:r0 + chunk, :] = a
        b_s[r0:r0 + chunk, :] = jnp.sqrt(one_minus) * (i * xc)

    def scan_body(j, carry):
        h, p = carry
        a = a_s[pl.ds(j, SUB, stride=seg), :]
        b = b_s[pl.ds(j, SUB, stride=seg), :]
        h = a * h + b
        p = a * p
        a_s[pl.ds(j, SUB, stride=seg), :] = p
        b_s[pl.ds(j, SUB, stride=seg), :] = h
        return h, p

    h_end, p_end = lax.fori_loop(
        0, seg, scan_body, (jnp.zeros((SUB, LANE), F32), jnp.ones((SUB, LANE), F32)))

    state = jnp.zeros((1, LANE), F32)
    for s in range(SUB):
        r0 = s * seg
        for c0 in range(0, seg, chunk):
            lo, n = r0 + c0, min(chunk, seg - c0)
            h = b_s[lo:lo + n, :] + a_s[lo:lo + n, :] * state
            o_ref[lo:lo + n, :] = h * _gelu_tanh(g_ref[lo:lo + n, :])
        state = h_end[s:s + 1, :] + p_end[s:s + 1, :] * state


def _rglru(z, conv_w, conv_b, w_rg_a, b_rg_a, w_rg_x, b_rg_x, lam, B, S, x_off, g_off):
    M = B * S
    NG = w_rg_a.shape[0]
    vec = lambda a: a.reshape(NG, 1, LANE)
    vec_spec = pl.BlockSpec((None, 1, LANE), lambda b, g: (g, 0, 0))
    mat_spec = pl.BlockSpec((None, LRU_BLOCK, LRU_BLOCK), lambda b, g: (g, 0, 0))
    cw = conv_w.reshape(CONV_WIDTH, NG, LANE).transpose(1, 0, 2)
    return pl.pallas_call(
        functools.partial(_lru_kernel, S=S, chunk=_tile(S // 8, 512)),
        grid=(B, NG),
        in_specs=[
            pl.BlockSpec((None, S, LANE), lambda b, g: (x_off + g, b, 0)),
            pl.BlockSpec((None, S, LANE), lambda b, g: (g_off + g, b, 0)),
            pl.BlockSpec((None, CONV_WIDTH, LANE), lambda b, g: (g, 0, 0)),
            vec_spec, mat_spec, vec_spec, mat_spec, vec_spec, vec_spec,
        ],
        out_specs=pl.BlockSpec((None, S, LANE), lambda b, g: (g, b, 0)),
        out_shape=jax.ShapeDtypeStruct((NG, M, LANE), F32),
        scratch_shapes=[pltpu.VMEM((S + 8, LANE), F32), pltpu.VMEM((S, LANE), F32),
                        pltpu.VMEM((S, LANE), F32)],
        compiler_params=_cparams("arbitrary", "arbitrary"),
        name="rglru",
    )(z, z, cw, vec(conv_b), w_rg_a, vec(b_rg_a), w_rg_x, vec(b_rg_x), vec(lam))


def _branch_norm_kernel(a_ref, l_ref, ga_ref, gl_ref, o_ref):
    col = 0
    for src, gain in ((a_ref, ga_ref), (l_ref, gl_ref)):
        nblk = src.shape[0]
        ssq = jnp.zeros((src.shape[1], 1), F32)
        for k in range(nblk):
            v = src[k]
            ssq = ssq + jnp.sum(v * v, axis=-1, keepdims=True)
        inv = lax.rsqrt(ssq / (nblk * LANE) + NORM_EPS)
        for k in range(nblk):
            o_ref[:, col:col + LANE] = (src[k] * inv * gain[:, k * LANE:(k + 1) * LANE]).astype(BF16)
            col += LANE


def _branch_norm(attn, lru, g_attn, g_lru):
    H, M, _ = attn.shape
    NG = lru.shape[0]
    D = (H + NG) * LANE
    tr = _tile(M, 256)
    return pl.pallas_call(
        _branch_norm_kernel,
        grid=(M // tr,),
        in_specs=[
            pl.BlockSpec((H, tr, LANE), lambda i: (0, i, 0)),
            pl.BlockSpec((NG, tr, LANE), lambda i: (0, i, 0)),
            pl.BlockSpec((1, H * LANE), lambda i: (0, 0)),
            pl.BlockSpec((1, NG * LANE), lambda i: (0, 0)),
        ],
        out_specs=pl.BlockSpec((tr, D), lambda i: (i, 0)),
        out_shape=jax.ShapeDtypeStruct((M, D), BF16),
        compiler_params=_cparams("arbitrary"),
        name="branch_norm",
    )(attn, lru, g_attn.reshape(1, -1), g_lru.reshape(1, -1))


def _final_norm_kernel(x_ref, g_ref, o_ref):
    x = x_ref[...]
    ms = jnp.mean(x * x, axis=-1, keepdims=True)
    o_ref[...] = x * lax.rsqrt(ms + NORM_EPS) * g_ref[...]


def _final_norm(x, g):
    M, D = x.shape
    tr = _tile(M, 256)
    return pl.pallas_call(
        _final_norm_kernel,
        grid=(M // tr,),
        in_specs=[pl.BlockSpec((tr, D), lambda i: (i, 0)), pl.BlockSpec((1, D), lambda i: (0, 0))],
        out_specs=pl.BlockSpec((tr, D), lambda i: (i, 0)),
        out_shape=jax.ShapeDtypeStruct((M, D), F32),
        compiler_params=_cparams("arbitrary"),
        name="final_norm",
    )(x, g.reshape(1, D))


def kernel(x, c, w_ada, b_ada, ada_emb, w_ffn1_gate, w_ffn1_up, w_ffn1_down, w_in, conv_w, conv_b,
           w_rg_a, b_rg_a, w_rg_x, b_rg_x, lru_lambda, g_attn_out, g_lru_out, w_out,
           w_ffn2_gate, w_ffn2_up, w_ffn2_down, g_final):
    B, S, D = x.shape
    depth = ada_emb.shape[0]
    attn_width = g_attn_out.shape[1]
    lru_width = g_lru_out.shape[1]
    H = attn_width // HEAD_DIM
    NG = lru_width // LRU_BLOCK
    assert w_rg_a.shape[1:] == (NG, LRU_BLOCK, LRU_BLOCK)
    assert S % (DILATIONS[-1] * 8) == 0 and S % Q_BLOCK == 0

    mods = _ada_mods(c, w_ada, b_ada, ada_emb)
    cos, sin = _rope_tables(S)
    bf = lambda w: w.astype(BF16)

    def ffn(xf, m, j, wg, wu, wd):
        h = _norm_mod(xf, m[:, j], m[:, j + 1], S)
        a = _swiglu_up(h, bf(wg), bf(wu))
        return _proj_residual(a, bf(wd), xf, m[:, j + 2], 0.5, S)

    xf = x.reshape(B * S, D)
    for l in range(depth):
        m = mods[l]
        xf = ffn(xf, m, 0, w_ffn1_gate[l], w_ffn1_up[l], w_ffn1_down[l])
        h = _norm_mod(xf, m[:, 3], m[:, 4], S)
        z = _in_proj(h, bf(w_in[l]))
        attn = _attention(z, cos, sin, B, S, H)
        lru = _rglru(z, conv_w[l], conv_b[l], w_rg_a[l], b_rg_a[l], w_rg_x[l], b_rg_x[l],
                     lru_lambda[l], B, S, 3 * H, 3 * H + NG)
        y = _branch_norm(attn, lru, g_attn_out[l], g_lru_out[l])
        xf = _proj_residual(y, bf(w_out[l]), xf, m[:, 5], 1.0, S)
        xf = ffn(xf, m, 6, w_ffn2_gate[l], w_ffn2_up[l], w_ffn2_down[l])
    return _final_norm(xf, g_final).reshape(B, S, D)
```

```python
import functools
import math

import jax
import jax.numpy as jnp
from jax import lax
from jax.experimental import pallas as pl
from jax.experimental.pallas import tpu as pltpu

F32 = jnp.float32
BF16 = jnp.bfloat16

HEAD_DIM = 128
LRU_BLOCK = 128
CONV_WIDTH = 4
LRU_C = 8.0
DILATIONS = (1, 4, 16)
Q_BLOCK = 128
ROPE_THETA = 10000.0
NORM_EPS = 1e-6
N_MOD = 9

LANE = 128
SUBLANE = 8
VMEM_LIMIT_BYTES = 56 * 1024 * 1024
NEG_BIG = -1e30
ATTN_GROUP = 8


def _cparams(*semantics):
    return pltpu.CompilerParams(dimension_semantics=semantics,
                                vmem_limit_bytes=VMEM_LIMIT_BYTES)


def _tile(n, pref):
    if n <= pref:
        return n
    t = (pref // LANE) * LANE
    while n % t:
        t -= LANE
    return t


def _ada_kernel(c_ref, w_ref, b_ref, e_ref, o_ref, *, depth):
    c = c_ref[...]
    sc = (c * jax.nn.sigmoid(c)).astype(BF16)
    acc = jnp.dot(sc, w_ref[...].astype(BF16), preferred_element_type=F32)
    base = acc + b_ref[...]
    for l in range(depth):
        o_ref[l] = base + e_ref[l:l + 1, :]


def _ada_mods(c, w_ada, b_ada, ada_emb):
    B, D = c.shape
    depth = ada_emb.shape[0]
    N = w_ada.shape[1]
    rows = 16
    c_pad = jnp.pad(c, ((0, rows - B), (0, 0)))
    tn = _tile(N, 512)
    out = pl.pallas_call(
        functools.partial(_ada_kernel, depth=depth),
        grid=(N // tn,),
        in_specs=[
            pl.BlockSpec((rows, D), lambda j: (0, 0)),
            pl.BlockSpec((D, tn), lambda j: (0, j)),
            pl.BlockSpec((1, tn), lambda j: (0, j)),
            pl.BlockSpec((depth, tn), lambda j: (0, j)),
        ],
        out_specs=pl.BlockSpec((depth, rows, tn), lambda j: (0, 0, j)),
        out_shape=jax.ShapeDtypeStruct((depth, rows, N), F32),
        compiler_params=_cparams("arbitrary"),
        name="ada_mods",
    )(c_pad, w_ada, b_ada.reshape(1, N), ada_emb.reshape(depth, N))
    return out[:, :B].reshape(depth, B, N_MOD, 1, D)


def _norm_mod_kernel(x_ref, sh_ref, sc_ref, o_ref):
    x = x_ref[...]
    ms = jnp.mean(x * x, axis=-1, keepdims=True)
    y = x * lax.rsqrt(ms + NORM_EPS)
    o_ref[...] = (y * (1.0 + sc_ref[...]) + sh_ref[...]).astype(BF16)


def _norm_mod(x, shift, scale, S):
    M, D = x.shape
    tr = _tile(S, 256)
    return pl.pallas_call(
        _norm_mod_kernel,
        grid=(M // tr,),
        in_specs=[
            pl.BlockSpec((tr, D), lambda i: (i, 0)),
            pl.BlockSpec((None, 1, D), lambda i: ((i * tr) // S, 0, 0)),
            pl.BlockSpec((None, 1, D), lambda i: ((i * tr) // S, 0, 0)),
        ],
        out_specs=pl.BlockSpec((tr, D), lambda i: (i, 0)),
        out_shape=jax.ShapeDtypeStruct((M, D), BF16),
        compiler_params=_cparams("arbitrary"),
        name="norm_mod",
    )(x, shift, scale)


def _cast_kernel(w_ref, o_ref):
    o_ref[...] = w_ref[...].astype(BF16)


def _cast_bf16(w):
    depth, K, N = w.shape
    tk, tn = _tile(K, 1024), _tile(N, 2048)
    return pl.pallas_call(
        _cast_kernel,
        grid=(depth, K // tk, N // tn),
        in_specs=[pl.BlockSpec((None, tk, tn), lambda l, i, j: (l, i, j))],
        out_specs=pl.BlockSpec((None, tk, tn), lambda l, i, j: (l, i, j)),
        out_shape=jax.ShapeDtypeStruct((depth, K, N), BF16),
        compiler_params=_cparams("arbitrary", "arbitrary", "arbitrary"),
        name="cast_bf16",
    )(w)


def _swiglu_up_kernel(h_ref, wg_ref, wu_ref, o_ref):
    h = h_ref[...]
    g = jnp.dot(h, wg_ref[...], preferred_element_type=F32)
    u = jnp.dot(h, wu_ref[...], preferred_element_type=F32)
    o_ref[...] = (g * jax.nn.sigmoid(g) * u).astype(BF16)


def _swiglu_up(h, wg, wu, l):
    M, D = h.shape
    F = wg.shape[2]
    tm, tn = _tile(M, 1024), _tile(F, 512)
    return pl.pallas_call(
        _swiglu_up_kernel,
        grid=(M // tm, F // tn),
        in_specs=[
            pl.BlockSpec((tm, D), lambda i, j: (i, 0)),
            pl.BlockSpec((None, D, tn), lambda i, j: (l, 0, j)),
            pl.BlockSpec((None, D, tn), lambda i, j: (l, 0, j)),
        ],
        out_specs=pl.BlockSpec((tm, tn), lambda i, j: (i, j)),
        out_shape=jax.ShapeDtypeStruct((M, F), BF16),
        compiler_params=_cparams("arbitrary", "arbitrary"),
        name="swiglu_up",
    )(h, wg, wu)


def _proj_residual_kernel(a_ref, w_ref, x_ref, g_ref, o_ref, *, coef):
    acc = jnp.dot(a_ref[...], w_ref[...], preferred_element_type=F32)
    o_ref[...] = x_ref[...] + (coef * g_ref[...]) * acc


def _proj_residual(a, w, l, x, gate, coef, S):
    M, K = a.shape
    N = w.shape[2]
    tm, tn = _tile(M, 1024), _tile(N, 512)
    tm = math.gcd(tm, S)
    return pl.pallas_call(
        functools.partial(_proj_residual_kernel, coef=coef),
        grid=(M // tm, N // tn),
        in_specs=[
            pl.BlockSpec((tm, K), lambda i, j: (i, 0)),
            pl.BlockSpec((None, K, tn), lambda i, j: (l, 0, j)),
            pl.BlockSpec((tm, tn), lambda i, j: (i, j)),
            pl.BlockSpec((None, 1, tn), lambda i, j: ((i * tm) // S, 0, j)),
        ],
        out_specs=pl.BlockSpec((tm, tn), lambda i, j: (i, j)),
        out_shape=jax.ShapeDtypeStruct((M, N), F32),
        compiler_params=_cparams("arbitrary", "arbitrary"),
        name="proj_residual",
    )(a, w, x, gate)


def _in_proj_kernel(h_ref, w_ref, o_ref, *, nsub):
    acc = jnp.dot(h_ref[...], w_ref[...], preferred_element_type=F32)
    for k in range(nsub):
        o_ref[k] = acc[:, k * LANE:(k + 1) * LANE]


def _in_proj(h, w, l):
    M, D = h.shape
    N = w.shape[2]
    tm, tn = _tile(M, 1024), _tile(N, 512)
    nsub = tn // LANE
    return pl.pallas_call(
        functools.partial(_in_proj_kernel, nsub=nsub),
        grid=(M // tm, N // tn),
        in_specs=[
            pl.BlockSpec((tm, D), lambda i, j: (i, 0)),
            pl.BlockSpec((None, D, tn), lambda i, j: (l, 0, j)),
        ],
        out_specs=pl.BlockSpec((nsub, tm, LANE), lambda i, j: (j, i, 0)),
        out_shape=jax.ShapeDtypeStruct((N // LANE, M, LANE), F32),
        compiler_params=_cparams("arbitrary", "arbitrary"),
        name="in_proj",
    )(h, w)


def _attn_kernel(q_ref, k_ref, v_ref, cos_ref, sin_ref, o_ref,
                 qn, kn, acc, mx, den, bias, s_scr, m_scr, p_scr, *, S, chunk):
    QB = Q_BLOCK
    nblk = S // QB
    scale = HEAD_DIM ** -0.5
    nt_dims = (((1,), (1,)), ((), ()))

    qi = lax.broadcasted_iota(jnp.int32, (QB, 2 * QB), 0)
    kj = lax.broadcasted_iota(jnp.int32, (QB, 2 * QB), 1)
    own = (kj >= QB) & (kj - QB <= qi)
    prev = (kj < QB) & (kj >= qi)
    bias[0] = jnp.where(own | prev, 0.0, NEG_BIG)
    bias[1] = jnp.where(own, 0.0, NEG_BIG)

    def rope_body(c, carry):
        r0 = pl.multiple_of(c * chunk, chunk)
        cs = cos_ref[pl.ds(r0, chunk), :]
        sn = sin_ref[pl.ds(r0, chunk), :]
        q = q_ref[pl.ds(r0, chunk), :]
        k = k_ref[pl.ds(r0, chunk), :]
        qn[pl.ds(r0, chunk), :] = (q * cs + pltpu.roll(q, HEAD_DIM // 2, 1) * sn) * scale
        kn[pl.ds(r0, chunk), :] = k * cs + pltpu.roll(k, HEAD_DIM // 2, 1) * sn
        return carry

    lax.fori_loop(0, S // chunk, rope_body, 0)

    ones_cols = jnp.ones((2 * QB, LANE), BF16)

    def rows(d, start):
        return pl.ds(start, QB) if d == 1 else pl.ds(start, QB, stride=d)

    def run(cfg, d, row0, n, first_has_prev, first_bias, slot0):
        step = QB * d
        kb, vb = {}, {}
        if first_has_prev:
            pstart = jnp.maximum(row0 - step, 0)
            kb[-1] = kn[rows(d, pstart), :].astype(BF16)
            vb[-1] = v_ref[rows(d, pstart), :].astype(BF16)
        for j in range(n):
            kb[j] = kn[rows(d, row0 + j * step), :].astype(BF16)
            vb[j] = v_ref[rows(d, row0 + j * step), :].astype(BF16)
        has_prev = [first_has_prev] + [True] * (n - 1)

        for j in range(n):
            slot = slot0 + j
            q = qn[rows(d, row0 + j * step), :].astype(BF16)
            if has_prev[j]:
                kcat = jnp.concatenate([kb[j - 1], kb[j]], axis=0)
                s = lax.dot_general(q, kcat, nt_dims, preferred_element_type=F32)
                s = s + (bias[first_bias] if j == 0 else bias[0])
                m = jnp.max(jnp.maximum(s[:, :QB], s[:, QB:]), axis=-1, keepdims=True)
                s_scr[slot] = s
            else:
                s = lax.dot_general(q, kb[j], nt_dims, preferred_element_type=F32)
                s = s + bias[1, :, QB:]
                m = jnp.max(s, axis=-1, keepdims=True)
                s_scr[slot, :, QB:] = s
            m_scr[slot] = jnp.broadcast_to(m, (QB, LANE))

        for j in range(n):
            slot = slot0 + j
            m = m_scr[slot]
            if has_prev[j]:
                p_scr[slot, :, :QB] = jnp.exp(s_scr[slot, :, :QB] - m).astype(BF16)
            p_scr[slot, :, QB:] = jnp.exp(s_scr[slot, :, QB:] - m).astype(BF16)

        for j in range(n):
            slot = slot0 + j
            if has_prev[j]:
                p = p_scr[slot]
                vext = jnp.concatenate(
                    [jnp.concatenate([vb[j - 1], vb[j]], axis=0), ones_cols], axis=1)
            else:
                p = p_scr[slot, :, QB:]
                vext = jnp.concatenate([vb[j], ones_cols[:QB]], axis=1)
            oe = jnp.dot(p, vext, preferred_element_type=F32)
            dst = rows(d, row0 + j * step)
            acc[cfg, dst, :] = oe[:, :LANE]
            den[cfg, dst, :] = oe[:, LANE:]
            mx[cfg, dst, :] = m_scr[slot]

    for cfg, d in enumerate(DILATIONS):
        per_class = nblk // d
        if d == 1:
            n = min(ATTN_GROUP, per_class)

            def body(c, carry, n=n, cfg=cfg):
                row0 = pl.multiple_of(c * (n * QB), n * QB)
                run(cfg, 1, row0, n, True, jnp.where(c == 0, 1, 0), 0)
                return carry

            lax.fori_loop(0, per_class // n, body, 0)
        else:
            assert 1 <= per_class <= ATTN_GROUP and ATTN_GROUP % per_class == 0
            classes = min(ATTN_GROUP // per_class, d)

            def body(c, carry, d=d, cfg=cfg, per_class=per_class, classes=classes):
                for ri in range(classes):
                    run(cfg, d, c * classes + ri, per_class, False, 1, ri * per_class)
                return carry

            lax.fori_loop(0, d // classes, body, 0)

    def mix_body(c, carry):
        rs = pl.ds(pl.multiple_of(c * chunk, chunk), chunk)
        ms = [mx[i, rs, :] for i in range(len(DILATIONS))]
        top = functools.reduce(jnp.maximum, ms)
        num = jnp.zeros((chunk, LANE), F32)
        dn = jnp.zeros((chunk, LANE), F32)
        for i in range(len(DILATIONS)):
            e = jnp.exp(ms[i] - top)
            num = num + e * acc[i, rs, :]
            dn = dn + e * den[i, rs, :]
        o_ref[rs, :] = num / dn
        return carry

    lax.fori_loop(0, S // chunk, mix_body, 0)


def _rope_tables(S):
    half = HEAD_DIM // 2
    inv = ROPE_THETA ** (-jnp.arange(half, dtype=F32) / half)
    ang = jnp.arange(S, dtype=F32)[:, None] * inv[None, :]
    cos, sin = jnp.cos(ang), jnp.sin(ang)
    return jnp.concatenate([cos, cos], axis=-1), jnp.concatenate([-sin, sin], axis=-1)


def _attention(z, cos, sin, B, S, H):
    M = B * S
    ncfg = len(DILATIONS)
    seq_spec = lambda off: pl.BlockSpec((None, S, LANE), lambda b, h: (off + h, b, 0))
    tab_spec = pl.BlockSpec((S, LANE), lambda b, h: (0, 0), pipeline_mode=pl.Buffered(1))
    scratch = [
        pltpu.VMEM((S, LANE), F32), pltpu.VMEM((S, LANE), F32),
        pltpu.VMEM((ncfg, S, LANE), F32), pltpu.VMEM((ncfg, S, LANE), F32),
        pltpu.VMEM((ncfg, S, LANE), F32),
        pltpu.VMEM((2, Q_BLOCK, 2 * Q_BLOCK), F32),
        pltpu.VMEM((ATTN_GROUP, Q_BLOCK, 2 * Q_BLOCK), F32),
        pltpu.VMEM((ATTN_GROUP, Q_BLOCK, LANE), F32),
        pltpu.VMEM((ATTN_GROUP, Q_BLOCK, 2 * Q_BLOCK), BF16),
    ]
    return pl.pallas_call(
        functools.partial(_attn_kernel, S=S, chunk=_tile(S, 256)),
        grid=(B, H),
        in_specs=[seq_spec(0), seq_spec(H), seq_spec(2 * H), tab_spec, tab_spec],
        out_specs=pl.BlockSpec((None, S, LANE), lambda b, h: (h, b, 0)),
        out_shape=jax.ShapeDtypeStruct((H, M, LANE), F32),
        scratch_shapes=scratch,
        compiler_params=_cparams("arbitrary", "arbitrary"),
        name="dilated_attention",
    )(z, z, z, cos, sin)


def _gelu_tanh(x):
    y = math.sqrt(2.0 / math.pi) * (x + 0.044715 * (x * x * x))
    return x * jax.nn.sigmoid(2.0 * y)


def _seg_pitch(seg):
    return seg if (seg // SUBLANE) % 2 else seg + SUBLANE


def _lru_kernel(x_ref, g_ref, cw_ref, cb_ref, wa_ref, ba_ref, wx_ref, bx_ref, lam_ref, o_ref,
                xpad, a_s, b_s, *, S):
    SUB = SUBLANE
    seg = S // SUB
    pitch = _seg_pitch(seg)
    xpad[0:SUB, :] = jnp.zeros((SUB, LANE), F32)
    xpad[SUB:SUB + S, :] = x_ref[...]

    lam = lam_ref[...]
    sp = jnp.maximum(-lam, 0.0) + jnp.log1p(jnp.exp(-jnp.abs(lam)))
    neg_c_sp = -LRU_C * sp
    wa = wa_ref[...].astype(BF16)
    wx = wx_ref[...].astype(BF16)
    cw = cw_ref[...]

    for c in range(SUB):
        r0 = c * seg
        xc = cb_ref[...] + jnp.zeros((seg, LANE), F32)
        for tap in range(CONV_WIDTH):
            off = SUB + r0 - (CONV_WIDTH - 1) + tap
            xc = xc + xpad[off:off + seg, :] * cw[tap:tap + 1, :]
        xcb = xc.astype(BF16)
        r = jax.nn.sigmoid(jnp.dot(xcb, wa, preferred_element_type=F32) + ba_ref[...])
        i = jax.nn.sigmoid(jnp.dot(xcb, wx, preferred_element_type=F32) + bx_ref[...])
        log_a = neg_c_sp * r
        a = jnp.exp(log_a)
        t = 2.0 * log_a
        series = -t * (1.0 + t * (0.5 + t * (1.0 / 6.0)))
        one_minus = jnp.where(t > -0.004, series, 1.0 - a * a)
        root = one_minus * lax.rsqrt(jnp.maximum(one_minus, 1e-30))
        a_s[c * pitch:c * pitch + seg, :] = a
        b_s[c * pitch:c * pitch + seg, :] = root * (i * xc)

    def scan_body(j, carry):
        h, p = carry
        a = a_s[pl.ds(j, SUB, stride=pitch), :]
        b = b_s[pl.ds(j, SUB, stride=pitch), :]
        h = a * h + b
        p = a * p
        a_s[pl.ds(j, SUB, stride=pitch), :] = p
        b_s[pl.ds(j, SUB, stride=pitch), :] = h
        return h, p

    h_end, p_end = lax.fori_loop(
        0, seg, scan_body, (jnp.zeros((SUB, LANE), F32), jnp.ones((SUB, LANE), F32)),
        unroll=8)

    state = jnp.zeros((1, LANE), F32)
    for s in range(SUB):
        lo = s * pitch
        h = b_s[lo:lo + seg, :] + a_s[lo:lo + seg, :] * state
        o_ref[s * seg:(s + 1) * seg, :] = h * _gelu_tanh(g_ref[s * seg:(s + 1) * seg, :])
        state = h_end[s:s + 1, :] + p_end[s:s + 1, :] * state


def _rglru(z, conv_w, conv_b, w_rg_a, b_rg_a, w_rg_x, b_rg_x, lam, B, S, x_off, g_off):
    M = B * S
    NG = w_rg_a.shape[0]
    vec = lambda a: a.reshape(NG, 1, LANE)
    vec_spec = pl.BlockSpec((None, 1, LANE), lambda b, g: (g, 0, 0))
    mat_spec = pl.BlockSpec((None, LRU_BLOCK, LRU_BLOCK), lambda b, g: (g, 0, 0))
    cw = conv_w.reshape(CONV_WIDTH, NG, LANE).transpose(1, 0, 2)
    scan_rows = SUBLANE * _seg_pitch(S // SUBLANE)
    return pl.pallas_call(
        functools.partial(_lru_kernel, S=S),
        grid=(B, NG),
        in_specs=[
            pl.BlockSpec((None, S, LANE), lambda b, g: (x_off + g, b, 0)),
            pl.BlockSpec((None, S, LANE), lambda b, g: (g_off + g, b, 0)),
            pl.BlockSpec((None, CONV_WIDTH, LANE), lambda b, g: (g, 0, 0)),
            vec_spec, mat_spec, vec_spec, mat_spec, vec_spec, vec_spec,
        ],
        out_specs=pl.BlockSpec((None, S, LANE), lambda b, g: (g, b, 0)),
        out_shape=jax.ShapeDtypeStruct((NG, M, LANE), F32),
        scratch_shapes=[pltpu.VMEM((S + SUBLANE, LANE), F32), pltpu.VMEM((scan_rows, LANE), F32),
                        pltpu.VMEM((scan_rows, LANE), F32)],
        compiler_params=_cparams("arbitrary", "arbitrary"),
        name="rglru",
    )(z, z, cw, vec(conv_b), w_rg_a, vec(b_rg_a), w_rg_x, vec(b_rg_x), vec(lam))


def _branch_norm_kernel(a_ref, l_ref, ga_ref, gl_ref, o_ref):
    col = 0
    for src, gain in ((a_ref, ga_ref), (l_ref, gl_ref)):
        nblk = src.shape[0]
        ssq = jnp.zeros((src.shape[1], 1), F32)
        for k in range(nblk):
            v = src[k]
            ssq = ssq + jnp.sum(v * v, axis=-1, keepdims=True)
        inv = lax.rsqrt(ssq / (nblk * LANE) + NORM_EPS)
        for k in range(nblk):
            o_ref[:, col:col + LANE] = (src[k] * inv * gain[:, k * LANE:(k + 1) * LANE]).astype(BF16)
            col += LANE


def _branch_norm(attn, lru, g_attn, g_lru):
    H, M, _ = attn.shape
    NG = lru.shape[0]
    D = (H + NG) * LANE
    tr = _tile(M, 256)
    return pl.pallas_call(
        _branch_norm_kernel,
        grid=(M // tr,),
        in_specs=[
            pl.BlockSpec((H, tr, LANE), lambda i: (0, i, 0)),
            pl.BlockSpec((NG, tr, LANE), lambda i: (0, i, 0)),
            pl.BlockSpec((1, H * LANE), lambda i: (0, 0)),
            pl.BlockSpec((1, NG * LANE), lambda i: (0, 0)),
        ],
        out_specs=pl.BlockSpec((tr, D), lambda i: (i, 0)),
        out_shape=jax.ShapeDtypeStruct((M, D), BF16),
        compiler_params=_cparams("arbitrary"),
        name="branch_norm",
    )(attn, lru, g_attn.reshape(1, -1), g_lru.reshape(1, -1))


def _final_norm_kernel(x_ref, g_ref, o_ref):
    x = x_ref[...]
    ms = jnp.mean(x * x, axis=-1, keepdims=True)
    o_ref[...] = x * lax.rsqrt(ms + NORM_EPS) * g_ref[...]


def _final_norm(x, g):
    M, D = x.shape
    tr = _tile(M, 256)
    return pl.pallas_call(
        _final_norm_kernel,
        grid=(M // tr,),
        in_specs=[pl.BlockSpec((tr, D), lambda i: (i, 0)), pl.BlockSpec((1, D), lambda i: (0, 0))],
        out_specs=pl.BlockSpec((tr, D), lambda i: (i, 0)),
        out_shape=jax.ShapeDtypeStruct((M, D), F32),
        compiler_params=_cparams("arbitrary"),
        name="final_norm",
    )(x, g.reshape(1, D))


def kernel(x, c, w_ada, b_ada, ada_emb, w_ffn1_gate, w_ffn1_up, w_ffn1_down, w_in, conv_w, conv_b,
           w_rg_a, b_rg_a, w_rg_x, b_rg_x, lru_lambda, g_attn_out, g_lru_out, w_out,
           w_ffn2_gate, w_ffn2_up, w_ffn2_down, g_final):
    B, S, D = x.shape
    depth = ada_emb.shape[0]
    attn_width = g_attn_out.shape[1]
    lru_width = g_lru_out.shape[1]
    H = attn_width // HEAD_DIM
    NG = lru_width // LRU_BLOCK
    assert w_rg_a.shape[1:] == (NG, LRU_BLOCK, LRU_BLOCK)
    assert S % (DILATIONS[-1] * Q_BLOCK) == 0 and S % (SUBLANE * SUBLANE) == 0

    mods = _ada_mods(c, w_ada, b_ada, ada_emb)
    cos, sin = _rope_tables(S)
    ffn1 = tuple(_cast_bf16(w) for w in (w_ffn1_gate, w_ffn1_up, w_ffn1_down))
    ffn2 = tuple(_cast_bf16(w) for w in (w_ffn2_gate, w_ffn2_up, w_ffn2_down))
    w_in_b, w_out_b = _cast_bf16(w_in), _cast_bf16(w_out)

    def ffn(xf, m, j, ws, l):
        h = _norm_mod(xf, m[:, j], m[:, j + 1], S)
        a = _swiglu_up(h, ws[0], ws[1], l)
        return _proj_residual(a, ws[2], l, xf, m[:, j + 2], 0.5, S)

    xf = x.reshape(B * S, D)
    for l in range(depth):
        m = mods[l]
        xf = ffn(xf, m, 0, ffn1, l)
        h = _norm_mod(xf, m[:, 3], m[:, 4], S)
        z = _in_proj(h, w_in_b, l)
        attn = _attention(z, cos, sin, B, S, H)
        lru = _rglru(z, conv_w[l], conv_b[l], w_rg_a[l], b_rg_a[l], w_rg_x[l], b_rg_x[l],
                     lru_lambda[l], B, S, 3 * H, 3 * H + NG)
        y = _branch_norm(attn, lru, g_attn_out[l], g_lru_out[l])
        xf = _proj_residual(y, w_out_b, l, xf, m[:, 5], 1.0, S)
        xf = ffn(xf, m, 6, ffn2, l)
    return _final_norm(xf, g_final).reshape(B, S, D)
```
